```python
import math
import jax, jax.numpy as jnp
from jax import lax
import numpy as np

D_MODEL = 1024
BATCH = 2
SEQ = 8192
DEPTH = 1
DEC_BATCH = 32
DEC_SEQ = 1
PAST_LEN = 8192
PAGE_SIZE = 128

HEAD_DIM = 64
N_ATT_HEADS = 8
N_REC_HEADS = 8
REC_DK = 64
REC_DV = 64
D_ATT = N_ATT_HEADS * HEAD_DIM
D_REC_K = N_REC_HEADS * REC_DK
D_REC_V = N_REC_HEADS * REC_DV
D_MIX = D_ATT + D_REC_V
IN_SIZES = (D_ATT, D_ATT, D_ATT, D_REC_K, D_REC_K, D_REC_V, D_REC_V)
D_IN = sum(IN_SIZES)
D_FF = 4 * D_MODEL
ROPE_THETA = 500000.0
ROT_DIM = HEAD_DIM // 4
MOBA_BLOCK = 256
MOBA_TOPK = 3
Q_CHUNK = 32
REC_CHUNK = 64
EPS = 1e-6

kernel_name = "hymba_moba_hgrn2_decode_step"


def rms_norm(x, g):
    xf = x.astype(jnp.float32)
    y = xf * lax.rsqrt(jnp.mean(xf * xf, axis=-1, keepdims=True) + EPS)
    return (y * g.astype(jnp.float32)).astype(x.dtype)


def partial_rope(x, pos):
    half = ROT_DIM // 2
    inv = jnp.exp(-math.log(ROPE_THETA) * jnp.arange(half, dtype=jnp.float32) * (2.0 / ROT_DIM))
    ang = pos.astype(jnp.float32)[:, None] * inv[None, :]
    cos = jnp.cos(ang)[None, :, None, :]
    sin = jnp.sin(ang)[None, :, None, :]
    xr = x[..., :ROT_DIM].astype(jnp.float32)
    x1, x2 = xr[..., :half], xr[..., half:]
    rot = jnp.concatenate([x1 * cos - x2 * sin, x2 * cos + x1 * sin], axis=-1).astype(x.dtype)
    return jnp.concatenate([rot, x[..., ROT_DIM:]], axis=-1)


def moba_attention(q, k, v, q_pos):
    B, NQ, H, dh = q.shape
    T = k.shape[1]
    NB = T // MOBA_BLOCK
    topk = min(MOBA_TOPK, NB)
    kb = k.reshape(B, NB, MOBA_BLOCK, H, dh)
    vb = v.reshape(B, NB, MOBA_BLOCK, H, dh)
    kmean = jnp.mean(kb.astype(jnp.float32), axis=2)
    scale = HEAD_DIM ** -0.5
    c = math.gcd(NQ, Q_CHUNK)
    bi = jnp.arange(B)[:, None, None, None]
    hi = jnp.arange(H)[None, None, :, None]
    blk_ids = jnp.arange(NB, dtype=jnp.int32)
    offs = jnp.arange(MOBA_BLOCK, dtype=jnp.int32)

    def one_chunk(args):
        qc, pc = args
        own = pc // MOBA_BLOCK
        gate = jnp.einsum('bqhd,bnhd->bqhn', qc.astype(jnp.float32), kmean)
        past = blk_ids[None, :] < own[:, None]
        gate = jnp.where(past[None, :, None, :], gate, -jnp.inf)
        _, sel = lax.top_k(gate, topk)
        sel_ok = jnp.arange(topk)[None, :] < own[:, None]
        own_b = jnp.broadcast_to(own[None, :, None, None], (B, c, H, 1)).astype(sel.dtype)
        blocks = jnp.concatenate([sel, own_b], axis=-1)
        ks = kb[bi, blocks, :, hi]
        vs = vb[bi, blocks, :, hi]
        s = jnp.einsum('bqhd,bqhnkd->bqhnk', qc, ks, preferred_element_type=jnp.float32) * scale
        keypos = blocks[..., None] * MOBA_BLOCK + offs
        slot_ok = jnp.concatenate([sel_ok, jnp.ones((c, 1), dtype=bool)], axis=-1)
        mask = slot_ok[None, :, None, :, None] & (keypos <= pc[None, :, None, None, None])
        s = jnp.where(mask, s, -jnp.inf)
        p = jax.nn.softmax(s.reshape(B, c, H, -1), axis=-1).reshape(s.shape)
        return jnp.einsum('bqhnk,bqhnkd->bqhd', p.astype(v.dtype), vs)

    qs = q.reshape(B, NQ // c, c, H, dh).transpose(1, 0, 2, 3, 4)
    ps = q_pos.reshape(NQ // c, c)
    out = lax.map(one_chunk, (qs, ps))
    return out.transpose(1, 0, 2, 3, 4).reshape(B, NQ, H, dh)


def hgrn2_chunk(S, inp):
    q, kk, lf, v = inp
    C = q.shape[2]
    b = jnp.cumsum(lf, axis=2)
    o_inter = jnp.einsum('bhtk,bhkv->bhtv', q * jnp.exp(b), S)
    causal = jnp.tril(jnp.ones((C, C), dtype=bool))
    diff = b[:, :, :, None, :] - b[:, :, None, :, :]
    decay = jnp.exp(jnp.where(causal[:, :, None], diff, -jnp.inf))
    a = jnp.einsum('bhtk,bhsk,bhtsk->bhts', q, kk, decay)
    o = o_inter + jnp.einsum('bhts,bhsv->bhtv', a, v)
    b_last = b[:, :, -1:, :]
    S_new = jnp.exp(b_last[:, :, 0, :])[..., None] * S + jnp.einsum('bhsk,bhsv->bhkv', kk * jnp.exp(b_last - b), v)
    return S_new, o


def hgrn2_scan(q, kk, lf, v, S0):
    B, T, H, _ = q.shape
    c = math.gcd(T, REC_CHUNK)
    n = T // c

    def to_chunks(a):
        return a.reshape(B, n, c, H, a.shape[-1]).transpose(1, 0, 3, 2, 4)

    S, o = lax.scan(hgrn2_chunk, S0, (to_chunks(q), to_chunks(kk), to_chunks(lf), to_chunks(v)))
    o = o.transpose(1, 0, 3, 2, 4).reshape(B, T, H, v.shape[-1])
    return o, S


def mixer(a, kpast, vpast, S0, lb, w_in, hgrn_norm_g, w_out):
    B, T, _ = a.shape
    P = kpast.shape[1]
    z = a @ w_in
    qa, ka, va, qr, fr, ir, gr = jnp.split(z, list(np.cumsum(IN_SIZES)[:-1]), axis=-1)
    pos = P + jnp.arange(T, dtype=jnp.int32)
    qa = partial_rope(qa.reshape(B, T, N_ATT_HEADS, HEAD_DIM), pos)
    ka = partial_rope(ka.reshape(B, T, N_ATT_HEADS, HEAD_DIM), pos)
    va = va.reshape(B, T, N_ATT_HEADS, HEAD_DIM)
    total = P + T
    pad = (-total) % MOBA_BLOCK
    zpad = jnp.zeros((B, pad, N_ATT_HEADS, HEAD_DIM), ka.dtype)
    k_full = jnp.concatenate([kpast.astype(ka.dtype), ka, zpad], axis=1)
    v_full = jnp.concatenate([vpast.astype(va.dtype), va, zpad], axis=1)
    o_att = moba_attention(qa, k_full, v_full, pos).reshape(B, T, D_ATT)
    lbh = lb.reshape(N_REC_HEADS, REC_DK)
    f = lbh + (1.0 - lbh) * jax.nn.sigmoid(fr.astype(jnp.float32).reshape(B, T, N_REC_HEADS, REC_DK))
    o_rec, S_new = hgrn2_scan(qr.astype(jnp.float32).reshape(B, T, N_REC_HEADS, REC_DK),
                              1.0 - f, jnp.log(f),
                              ir.astype(jnp.float32).reshape(B, T, N_REC_HEADS, REC_DV),
                              S0.astype(jnp.float32))
    o_rec = rms_norm(o_rec, hgrn_norm_g) * jax.nn.silu(gr.astype(jnp.float32).reshape(B, T, N_REC_HEADS, REC_DV))
    o_rec = o_rec.astype(a.dtype).reshape(B, T, D_REC_V)
    out = jnp.concatenate([o_att, o_rec], axis=-1) @ w_out
    return out, ka, va, S_new


def layer(h, kpast, vpast, S0, lb, n1, w_in, hg, w_out, n2, w_up, w_down):
    mix, k_rows, v_rows, S_new = mixer(rms_norm(h, n1), kpast, vpast, S0, lb, w_in, hg, w_out)
    h = h + mix
    m = rms_norm(h, n2)
    h = h + jnp.square(jax.nn.relu(m @ w_up)) @ w_down
    return h, k_rows, v_rows, S_new


def setup_inputs(seed: int = 0) -> dict:
    key = jax.random.key(seed)
    ks = jax.random.split(key, 16)
    n_pages = PAST_LEN // PAGE_SIZE
    n_used = DEC_BATCH * n_pages
    n_pool = n_used + max(1, n_used // 4)
    f32 = jnp.float32
    x_prompt = jax.random.normal(ks[0], (BATCH, SEQ, D_MODEL), f32)
    x_sample = jax.random.normal(ks[1], (DEC_BATCH, DEC_SEQ, D_MODEL), f32)
    cache_k = jax.random.normal(ks[2], (DEPTH, n_pool, PAGE_SIZE, N_ATT_HEADS, HEAD_DIM), f32)
    cache_v = jax.random.normal(ks[3], (DEPTH, n_pool, PAGE_SIZE, N_ATT_HEADS, HEAD_DIM), f32)
    state_hgrn = 0.5 * jax.random.normal(ks[4], (DEPTH, DEC_BATCH, N_REC_HEADS, REC_DK, REC_DV), f32)
    page_table = jax.random.permutation(ks[5], n_pool)[:n_used].reshape(DEC_BATCH, n_pages).astype(jnp.int32)
    norm1_g = 1.0 + 0.02 * jax.random.normal(ks[6], (DEPTH, D_MODEL), f32)
    w_in = jax.random.normal(ks[7], (DEPTH, D_MODEL, D_IN), f32) * D_MODEL ** -0.5
    lb_logits = 0.1 * jax.random.normal(ks[8], (DEPTH + 1, D_REC_K), f32)
    hgrn_norm_g = 1.0 + 0.02 * jax.random.normal(ks[9], (DEPTH, REC_DV), f32)
    w_out = jax.random.normal(ks[10], (DEPTH, D_MIX, D_MODEL), f32) * D_MIX ** -0.5
    norm2_g = 1.0 + 0.02 * jax.random.normal(ks[11], (DEPTH, D_MODEL), f32)
    w_up = jax.random.normal(ks[12], (DEPTH, D_MODEL, D_FF), f32) * D_MODEL ** -0.5
    w_down = jax.random.normal(ks[13], (DEPTH, D_FF, D_MODEL), f32) * D_FF ** -0.5
    final_norm_g = 1.0 + 0.02 * jax.random.normal(ks[14], (D_MODEL,), f32)
    return {"x_prompt": x_prompt, "x_sample": x_sample, "cache_k": cache_k, "cache_v": cache_v,
            "state_hgrn": state_hgrn, "page_table": page_table, "norm1_g": norm1_g, "w_in": w_in,
            "lb_logits": lb_logits, "hgrn_norm_g": hgrn_norm_g, "w_out": w_out, "norm2_g": norm2_g,
            "w_up": w_up, "w_down": w_down, "final_norm_g": final_norm_g}


def reference(x_prompt, x_sample, cache_k, cache_v, state_hgrn, page_table, norm1_g, w_in, lb_logits,
              hgrn_norm_g, w_out, norm2_g, w_up, w_down, final_norm_g):
    lb_all = jnp.cumsum(jax.nn.softmax(lb_logits.astype(jnp.float32), axis=0), axis=0)
    n_pages = PAST_LEN // PAGE_SIZE
    Bp = x_prompt.shape[0]
    Bs = x_sample.shape[0]
    hp, hs = x_prompt, x_sample
    kp_l, vp_l, sp_l, ks_l, vs_l, ss_l = [], [], [], [], [], []
    for l in range(DEPTH):
        wargs = (lb_all[l], norm1_g[l], w_in[l], hgrn_norm_g[l], w_out[l], norm2_g[l], w_up[l], w_down[l])
        empty = jnp.zeros((Bp, 0, N_ATT_HEADS, HEAD_DIM), hp.dtype)
        S0p = jnp.zeros((Bp, N_REC_HEADS, REC_DK, REC_DV), jnp.float32)
        hp, kr, vr, Sp = layer(hp, empty, empty, S0p, *wargs)
        kp_l.append(kr); vp_l.append(vr); sp_l.append(Sp)
        kpast = cache_k[l][page_table].reshape(Bs, n_pages * PAGE_SIZE, N_ATT_HEADS, HEAD_DIM)
        vpast = cache_v[l][page_table].reshape(Bs, n_pages * PAGE_SIZE, N_ATT_HEADS, HEAD_DIM)
        hs, kr, vr, Ss = layer(hs, kpast, vpast, state_hgrn[l], *wargs)
        ks_l.append(kr); vs_l.append(vr); ss_l.append(Ss)
    y_prompt = rms_norm(hp, final_norm_g)
    y_sample = rms_norm(hs, final_norm_g)
    return (y_prompt, y_sample, jnp.stack(kp_l), jnp.stack(vp_l), jnp.stack(sp_l),
            jnp.stack(ks_l), jnp.stack(vs_l), jnp.stack(ss_l))
```

```python
import functools
import math

import jax
import jax.numpy as jnp
from jax import lax
from jax.experimental import pallas as pl
from jax.experimental.pallas import tpu as pltpu

F32 = jnp.float32
BF16 = jnp.bfloat16

D_MODEL = 1024
HEAD_DIM = 64
N_HEADS = 8
D_HEADS = N_HEADS * HEAD_DIM
N_SECTIONS = 7
D_FF = 4 * D_MODEL
ROPE_THETA = 500000.0
ROT_DIM = HEAD_DIM // 4
MOBA_BLOCK = 256
MOBA_TOPK = 3
EPS = 1e-6

LANES = 128
HEAD_PAIR = 2 * HEAD_DIM
NEG = -2.0 ** 30
REC_CHUNK = 32
REC_TILE = 256
REC_GROUP = 4 * HEAD_DIM
EXP_CLAMP = 80.0
VMEM_LIMIT = 56 * 1024 * 1024


def _cparams(*sem):
    return pltpu.CompilerParams(dimension_semantics=sem, vmem_limit_bytes=VMEM_LIMIT)


def _split2(x):
    hi = x.astype(BF16)
    lo = (x - hi.astype(F32)).astype(BF16)
    return hi, lo


def _split3(x):
    hi = x.astype(BF16)
    r = x - hi.astype(F32)
    mid = r.astype(BF16)
    lo = (r - mid.astype(F32)).astype(BF16)
    return hi, mid, lo


def _dot(a, b):
    return jnp.dot(a, b, preferred_element_type=F32)


def _dot_nt(a, b):
    return lax.dot_general(a, b, (((1,), (1,)), ((), ())), preferred_element_type=F32)


def _dot_tn(a, b):
    return lax.dot_general(a, b, (((0,), (0,)), ((), ())), preferred_element_type=F32)


def _rms(x, g):
    ms = jnp.mean(x * x, axis=-1, keepdims=True)
    return x * lax.rsqrt(ms + EPS) * g


def _rope_tables(pos):
    half = ROT_DIM // 2
    inv = jnp.exp(-math.log(ROPE_THETA) * jnp.arange(half, dtype=F32) * (2.0 / ROT_DIM))
    ang = pos.astype(F32)[:, None] * inv[None, :]
    cos, sin = jnp.cos(ang), jnp.sin(ang)
    n = pos.shape[0]
    ones = jnp.ones((n, HEAD_DIM - ROT_DIM), F32)
    zeros = jnp.zeros((n, HEAD_DIM - ROT_DIM), F32)
    zh = jnp.zeros((n, half), F32)
    c = jnp.concatenate([cos, cos, ones], axis=1)
    s_lo = jnp.concatenate([zh, sin, zeros], axis=1)
    s_hi = jnp.concatenate([-sin, zh, zeros], axis=1)
    tab = jnp.stack([c, s_lo, s_hi])
    return jnp.concatenate([tab, tab], axis=-1)


def _inproj_kernel(x_ref, g_ref, w_ref, cs_ref, lb_ref, *out_refs, prompt):
    if prompt:
        (q_ref, krow_ref, vrow_ref, kb_ref, vt_ref, kmean_ref,
         qr_ref, lf_ref, kk_ref, ir_ref, sg_ref) = out_refs
    else:
        (q_ref, krow_ref, vrow_ref, qr_ref, lf_ref, kk_ref, ir_ref, sg_ref) = out_refs
    a = _rms(x_ref[...], g_ref[...]).astype(BF16)

    def sec(s):
        return _dot(a, w_ref[:, s * D_HEADS:(s + 1) * D_HEADS])

    reps = D_HEADS // LANES
    c = jnp.tile(cs_ref[0], (1, reps))
    s_lo = jnp.tile(cs_ref[1], (1, reps))
    s_hi = jnp.tile(cs_ref[2], (1, reps))
    half = ROT_DIM // 2

    def rope(z):
        return (z * c + pltpu.roll(z, half, 1) * s_lo
                + pltpu.roll(z, D_HEADS - half, 1) * s_hi)

    q_ref[...] = rope(sec(0))
    k = rope(sec(1))
    krow_ref[...] = k
    v = sec(2)
    vrow_ref[...] = v
    if prompt:
        kb_ref[...] = k.astype(BF16)
        tm = k.shape[0]
        nblk = tm // MOBA_BLOCK
        vt = v.T.astype(BF16)
        for blk in range(nblk):
            kmean_ref[0, blk:blk + 1, :] = jnp.sum(
                k[blk * MOBA_BLOCK:(blk + 1) * MOBA_BLOCK], axis=0, keepdims=True) * (1.0 / MOBA_BLOCK)
            for hp in range(D_HEADS // HEAD_PAIR):
                vt_ref[0, hp, blk] = vt[hp * HEAD_PAIR:(hp + 1) * HEAD_PAIR,
                                        blk * MOBA_BLOCK:(blk + 1) * MOBA_BLOCK]
    qr_ref[...] = sec(3)
    lb = lb_ref[...]
    f = lb + (1.0 - lb) * jax.nn.sigmoid(sec(4))
    lf_ref[...] = jnp.log(f)
    kk_ref[...] = 1.0 - f
    ir_ref[...] = sec(5)
    sg_ref[...] = jax.nn.silu(sec(6))


def _inproj(x, g1, w_in_bf, cs, lb, *, tm, n_pos_tiles, prompt, batch=1):
    n = x.shape[0]
    nt = n // tm
    row = lambda i: (i, 0)
    const = lambda i: (0, 0)
    in_specs = [
        pl.BlockSpec((tm, D_MODEL), row),
        pl.BlockSpec((1, D_MODEL), const),
        pl.BlockSpec((D_MODEL, N_SECTIONS * D_HEADS), const),
        pl.BlockSpec((3, tm, LANES), lambda i: (0, i % n_pos_tiles, 0)),
        pl.BlockSpec((1, D_HEADS), const),
    ]
    f32_rows = jax.ShapeDtypeStruct((n, D_HEADS), F32)
    rows_spec = pl.BlockSpec((tm, D_HEADS), row)
    if prompt:
        nblk = tm // MOBA_BLOCK
        t = n // batch
        npair = D_HEADS // HEAD_PAIR
        out_shape = [f32_rows, f32_rows, f32_rows,
                     jax.ShapeDtypeStruct((n, D_HEADS), BF16),
                     jax.ShapeDtypeStruct((batch, npair, t // MOBA_BLOCK, HEAD_PAIR, MOBA_BLOCK), BF16),
                     jax.ShapeDtypeStruct((nt, nblk, D_HEADS), F32)] + [f32_rows] * 5
        out_specs = [rows_spec, rows_spec, rows_spec, rows_spec,
                     pl.BlockSpec((1, npair, nblk, HEAD_PAIR, MOBA_BLOCK),
                                  lambda i: (i // n_pos_tiles, 0, i % n_pos_tiles, 0, 0)),
                     pl.BlockSpec((1, nblk, D_HEADS), lambda i: (i, 0, 0))] + [rows_spec] * 5
    else:
        out_shape = [f32_rows] * 8
        out_specs = [rows_spec] * 8
    return pl.pallas_call(
        functools.partial(_inproj_kernel, prompt=prompt),
        grid=(nt,), in_specs=in_specs, out_specs=out_specs, out_shape=out_shape,
        compiler_params=_cparams("parallel"),
    )(x, g1, w_in_bf, cs, lb)


def _top3_rows(g, n_iota, n_valid):
    nb = g.shape[0]
    sel = jnp.zeros(g.shape, F32)
    for r in range(MOBA_TOPK):
        m = jnp.max(g, axis=0, keepdims=True)
        idx = jnp.min(jnp.where(g == m, n_iota, nb), axis=0, keepdims=True)
        hit = n_iota == idx
        sel = jnp.where(hit, jnp.where(n_valid > r, 1.0, sel), sel)
        g = jnp.where(hit, -jnp.inf, g)
    return sel > 0.5


def _gate_kernel(q_ref, km_ref, qt_ref, bias_ref, *, tq):
    i = pl.program_id(1)
    q = q_ref[...]
    km = km_ref[0]
    nb = km.shape[0]
    lane_head = lax.broadcasted_iota(jnp.int32, (nb, D_HEADS), 1) // HEAD_DIM
    stack = jnp.concatenate(
        [jnp.where(lane_head == h, km, 0.0) for h in range(N_HEADS)], axis=0)
    a_hi, a_lo = _split2(stack)
    b_hi, b_lo = _split2(q)
    g_all = _dot_nt(a_hi, b_hi) + _dot_nt(a_hi, b_lo) + _dot_nt(a_lo, b_hi)
    pos = i * tq + lax.broadcasted_iota(jnp.int32, (nb, tq), 1)
    own = pos // MOBA_BLOCK
    n_iota = lax.broadcasted_iota(jnp.int32, (nb, tq), 0)
    past = n_iota < own
    for h in range(N_HEADS):
        g = jnp.where(past, g_all[h * nb:(h + 1) * nb], -jnp.inf)
        sel = _top3_rows(g, n_iota, own)
        bias_ref[0, h] = jnp.where(sel, 0.0, NEG).astype(BF16)
    qt_ref[0] = (q * (HEAD_DIM ** -0.5)).T.astype(BF16)


def _gate(q, kmean, *, batch, tq):
    n = q.shape[0]
    t = n // batch
    ntq = t // tq
    nb = kmean.shape[1]
    return pl.pallas_call(
        functools.partial(_gate_kernel, tq=tq),
        grid=(batch, ntq),
        in_specs=[pl.BlockSpec((tq, D_HEADS), lambda b, i: (b * ntq + i, 0)),
                  pl.BlockSpec((1, nb, D_HEADS), lambda b, i: (b, 0, 0))],
        out_specs=[pl.BlockSpec((1, D_HEADS, tq), lambda b, i: (b, 0, i)),
                   pl.BlockSpec((1, N_HEADS, nb, tq), lambda b, i: (b, 0, 0, i))],
        out_shape=[jax.ShapeDtypeStruct((batch, D_HEADS, t), BF16),
                   jax.ShapeDtypeStruct((batch, N_HEADS, nb, t), BF16)],
        compiler_params=_cparams("parallel", "parallel"),
    )(q, kmean)


def _attn_kernel(qt_ref, bias_ref, k_ref, vt_ref, oh_ref, o_ref):
    i = pl.program_id(2)
    tq = qt_ref.shape[2]
    nb = bias_ref.shape[2]
    zq = jnp.zeros((HEAD_DIM, tq), BF16)
    zpad = jnp.zeros((2 * LANES - HEAD_PAIR - nb, tq), BF16)
    ws = []
    for hl in range(2):
        qh = qt_ref[0, hl * HEAD_DIM:(hl + 1) * HEAD_DIM, :]
        parts = [qh, zq] if hl == 0 else [zq, qh]
        ws.append(jnp.concatenate(parts + [bias_ref[0, hl], zpad], axis=0))

    def block_scores(j, hl, with_bias):
        kj = k_ref[pl.ds(pl.multiple_of(j * MOBA_BLOCK, MOBA_BLOCK), MOBA_BLOCK), :]
        if with_bias:
            lhs = jnp.concatenate([kj, oh_ref[j]], axis=1)
            return _dot(lhs, ws[hl])
        return _dot(kj, ws[hl][:HEAD_PAIR])

    def update(j, hl, s, m, l, acc):
        m_new = jnp.maximum(m, jnp.max(s, axis=0, keepdims=True))
        alpha = jnp.exp(m - m_new)
        p = jnp.exp(s - m_new)
        l_new = alpha * l + jnp.sum(p, axis=0, keepdims=True)
        vt = vt_ref[0, 0, j, hl * HEAD_DIM:(hl + 1) * HEAD_DIM, :]
        acc_new = alpha * acc + _dot(vt, p.astype(BF16))
        return m_new, l_new, acc_new

    r_iota = lax.broadcasted_iota(jnp.int32, (MOBA_BLOCK, tq), 0)
    c_iota = lax.broadcasted_iota(jnp.int32, (MOBA_BLOCK, tq), 1)
    causal = r_iota <= c_iota
    state = []
    for hl in range(2):
        s = jnp.where(causal, block_scores(i, hl, False), NEG)
        m0 = jnp.full((1, tq), NEG, F32)
        l0 = jnp.zeros((1, tq), F32)
        a0 = jnp.zeros((HEAD_DIM, tq), F32)
        state.extend(update(i, hl, s, m0, l0, a0))

    def body(j, st):
        out = []
        for hl in range(2):
            m, l, acc = st[3 * hl:3 * hl + 3]
            out.extend(update(j, hl, block_scores(j, hl, True), m, l, acc))
        return tuple(out)

    st = lax.fori_loop(0, i, body, tuple(state))
    o = jnp.concatenate([st[2] / st[1], st[5] / st[4]], axis=0)
    o_ref[...] = o.T.astype(BF16)


def _attention(qt, bias, kb, vt, *, batch):
    t = qt.shape[2]
    tq = MOBA_BLOCK
    nt = t // tq
    nb = bias.shape[2]
    npair = D_HEADS // HEAD_PAIR
    onehot = (jnp.arange(LANES)[None, None, :] == jnp.arange(nb)[:, None, None])
    onehot = jnp.broadcast_to(onehot, (nb, MOBA_BLOCK, LANES)).astype(BF16)
    return pl.pallas_call(
        _attn_kernel,
        grid=(batch, npair, nt),
        in_specs=[pl.BlockSpec((1, HEAD_PAIR, tq), lambda b, hp, i: (b, hp, i)),
                  pl.BlockSpec((1, 2, nb, tq), lambda b, hp, i: (b, hp, 0, i)),
                  pl.BlockSpec((t, HEAD_PAIR), lambda b, hp, i: (b, hp)),
                  pl.BlockSpec((1, 1, nt, HEAD_PAIR, MOBA_BLOCK), lambda b, hp, i: (b, hp, 0, 0, 0)),
                  pl.BlockSpec((nb, MOBA_BLOCK, LANES), lambda b, hp, i: (0, 0, 0))],
        out_specs=pl.BlockSpec((tq, HEAD_PAIR), lambda b, hp, i: (b * nt + i, hp)),
        out_shape=jax.ShapeDtypeStruct((batch * t, D_HEADS), BF16),
        compiler_params=_cparams("parallel", "parallel", "arbitrary"),
    )(qt, bias, kb, vt, onehot)


def _hgrn_kernel(q_ref, lf_ref, kk_ref, v_ref, sg_ref, g_ref, o_ref, st_ref, state):
    t = pl.program_id(2)
    n = REC_TILE

    @pl.when(t == 0)
    def _():
        state[...] = jnp.zeros_like(state)

    r_i = lax.broadcasted_iota(jnp.int32, (n, n), 0)
    c_i = lax.broadcasted_iota(jnp.int32, (n, n), 1)
    same_chunk = (r_i // REC_CHUNK) == (c_i // REC_CHUNK)
    tri = jnp.where(same_chunk & (c_i <= r_i), 1.0, 0.0).astype(BF16)
    ones_c = jnp.where(same_chunk, 1.0, 0.0).astype(BF16)
    same_head = (r_i // HEAD_DIM) == (c_i // HEAD_DIM)
    head_bd = jnp.where(same_head, 1.0, 0.0)

    q = q_ref[...]
    kk = kk_ref[...]
    v = v_ref[...]
    l_hi, l_mid, l_lo = _split3(lf_ref[...])
    b = _dot(tri, l_hi) + _dot(tri, l_mid) + _dot(tri, l_lo)
    btot = _dot(ones_c, l_hi) + _dot(ones_c, l_mid) + _dot(ones_c, l_lo)
    ref = 0.5 * btot
    qs = (q * jnp.exp(jnp.minimum(b - ref, EXP_CLAMP))).astype(BF16)
    ks = (kk * jnp.exp(jnp.minimum(ref - b, EXP_CLAMP))).astype(BF16)
    qe = (q * jnp.exp(b)).astype(BF16)
    kd = (kk * jnp.exp(btot - b)).astype(BF16)
    vb = v.astype(BF16)

    lane_head = lax.broadcasted_iota(jnp.int32, (n, REC_GROUP), 1) // HEAD_DIM
    causal = same_chunk & (c_i <= r_i)
    o = jnp.zeros((n, REC_GROUP), F32)
    for h in range(REC_GROUP // HEAD_DIM):
        a = _dot_nt(jnp.where(lane_head == h, qs, jnp.zeros_like(qs)), ks)
        a = jnp.where(causal, a, 0.0).astype(BF16)
        o = o + jnp.where(lane_head == h, _dot(a, vb), 0.0)

    s_t = state[...]
    outs = []
    for c in range(n // REC_CHUNK):
        rows = slice(c * REC_CHUNK, (c + 1) * REC_CHUNK)
        outs.append(_dot_nt(qe[rows], s_t.astype(BF16)))
        u_t = _dot_tn(vb[rows], kd[rows])
        decay = jnp.exp(btot[c * REC_CHUNK:c * REC_CHUNK + 1, :])
        s_t = s_t * decay + u_t * head_bd
    state[...] = s_t
    o = o + jnp.concatenate(outs, axis=0)

    sq_hi, sq_lo = _split2(o * o)
    bd = head_bd.astype(BF16)
    ms = (_dot(sq_hi, bd) + _dot(sq_lo, bd)) * (1.0 / HEAD_DIM)
    o_ref[...] = (o * lax.rsqrt(ms + EPS) * g_ref[...] * sg_ref[...]).astype(BF16)

    @pl.when(t == pl.num_programs(2) - 1)
    def _():
        st_ref[0, 0] = s_t


def _hgrn(qr, lf, kk, ir, sg, g_tiled, *, batch):
    n = qr.shape[0]
    t = n // batch
    nt = t // REC_TILE
    ng = D_HEADS // REC_GROUP
    blk = pl.BlockSpec((REC_TILE, REC_GROUP), lambda b, hg, i: (b * nt + i, hg))
    return pl.pallas_call(
        _hgrn_kernel,
        grid=(batch, ng, nt),
        in_specs=[blk, blk, blk, blk, blk,
                  pl.BlockSpec((1, REC_GROUP), lambda b, hg, i: (0, 0))],
        out_specs=[blk,
                   pl.BlockSpec((1, 1, REC_GROUP, REC_GROUP), lambda b, hg, i: (b, hg, 0, 0))],
        out_shape=[jax.ShapeDtypeStruct((n, D_HEADS), BF16),
                   jax.ShapeDtypeStruct((batch, ng, REC_GROUP, REC_GROUP), F32)],
        scratch_shapes=[pltpu.VMEM((REC_GROUP, REC_GROUP), F32)],
        compiler_params=_cparams("parallel", "parallel", "arbitrary"),
    )(qr, lf, kk, ir, sg, g_tiled)


def _unpack_state(st):
    b, ng = st.shape[:2]
    per = REC_GROUP // HEAD_DIM
    st = st.reshape(b, ng, per, HEAD_DIM, per, HEAD_DIM)
    diag = jnp.stack([st[:, :, h, :, h, :] for h in range(per)], axis=2)
    return diag.transpose(0, 1, 2, 4, 3).reshape(b, ng * per, HEAD_DIM, HEAD_DIM)


def _ffn_kernel(x_ref, oa_ref, orr_ref, wo_ref, n2_ref, wu_ref, wd_ref, gf_ref, y_ref,
                h_scr, m_scr, acc_scr):
    k = pl.program_id(1)

    @pl.when(k == 0)
    def _():
        h = (x_ref[...] + _dot(oa_ref[...], wo_ref[:D_HEADS, :])
             + _dot(orr_ref[...], wo_ref[D_HEADS:, :]))
        h_scr[...] = h
        m_scr[...] = _rms(h, n2_ref[...]).astype(BF16)
        acc_scr[...] = jnp.zeros_like(acc_scr)

    u = jnp.maximum(_dot(m_scr[...], wu_ref[...]), 0.0)
    acc_scr[...] += _dot((u * u).astype(BF16), wd_ref[...])

    @pl.when(k == pl.num_programs(1) - 1)
    def _():
        y_ref[...] = _rms(h_scr[...] + acc_scr[...], gf_ref[...])


def _ffn(x, o_att, o_rec, w_out_bf, n2, w_up_bf, w_down_bf, gf, *, tm, tf):
    n = x.shape[0]
    row = lambda i, k: (i, 0)
    const = lambda i, k: (0, 0)
    return pl.pallas_call(
        _ffn_kernel,
        grid=(n // tm, D_FF // tf),
        in_specs=[pl.BlockSpec((tm, D_MODEL), row),
                  pl.BlockSpec((tm, D_HEADS), row),
                  pl.BlockSpec((tm, D_HEADS), row),
                  pl.BlockSpec((2 * D_HEADS, D_MODEL), const),
                  pl.BlockSpec((1, D_MODEL), const),
                  pl.BlockSpec((D_MODEL, tf), lambda i, k: (0, k)),
                  pl.BlockSpec((tf, D_MODEL), lambda i, k: (k, 0)),
                  pl.BlockSpec((1, D_MODEL), const)],
        out_specs=pl.BlockSpec((tm, D_MODEL), row),
        out_shape=jax.ShapeDtypeStruct((n, D_MODEL), F32),
        scratch_shapes=[pltpu.VMEM((tm, D_MODEL), F32),
                        pltpu.VMEM((tm, D_MODEL), BF16),
                        pltpu.VMEM((tm, D_MODEL), F32)],
        compiler_params=_cparams("parallel", "arbitrary"),
    )(x, o_att, o_rec, w_out_bf, n2, w_up_bf, w_down_bf, gf)


KM_BUFS = 8


def _page_mean_kernel(pt_ref, cache_ref, km_ref, buf, sem, *, n_pages, pages_per_block):
    total = pt_ref.shape[0]
    page_rows = buf.shape[1]

    def copy(g, slot):
        return pltpu.make_async_copy(cache_ref.at[0, pt_ref[g]], buf.at[slot], sem.at[slot])

    for s in range(KM_BUFS):
        copy(s, s).start()
    km_ref[...] = jnp.zeros_like(km_ref)

    def body(g, carry):
        slot = g % KM_BUFS
        copy(g, slot).wait()
        part = jnp.sum(buf[slot], axis=0) * (1.0 / (page_rows * pages_per_block))

        @pl.when(g + KM_BUFS < total)
        def _():
            copy(g + KM_BUFS, slot).start()

        b = g // n_pages
        blk = (g % n_pages) // pages_per_block
        km_ref[b, blk] += part
        return carry

    lax.fori_loop(0, total, body, 0)


def _page_means(cache, page_table):
    bsz, n_pages = page_table.shape
    page_rows = cache.shape[2]
    ppb = MOBA_BLOCK // page_rows
    nb = n_pages // ppb
    return pl.pallas_call(
        functools.partial(_page_mean_kernel, n_pages=n_pages, pages_per_block=ppb),
        grid_spec=pltpu.PrefetchScalarGridSpec(
            num_scalar_prefetch=1, grid=(1,),
            in_specs=[pl.BlockSpec(memory_space=pl.ANY)],
            out_specs=pl.BlockSpec((bsz, nb, N_HEADS, HEAD_DIM), lambda i, pt: (0, 0, 0, 0)),
            scratch_shapes=[pltpu.VMEM((KM_BUFS, page_rows, N_HEADS, HEAD_DIM), F32),
                            pltpu.SemaphoreType.DMA((KM_BUFS,))]),
        out_shape=jax.ShapeDtypeStruct((bsz, nb, N_HEADS, HEAD_DIM), F32),
        compiler_params=_cparams("arbitrary"),
    )(page_table.reshape(-1), cache)


def _sample_gate_kernel(q_ref, km_ref, sel_ref):
    bsz, nb, _ = km_ref.shape
    lane_head = lax.broadcasted_iota(jnp.int32, (D_HEADS, LANES), 0) // HEAD_DIM
    col = lax.broadcasted_iota(jnp.int32, (D_HEADS, LANES), 1)
    ind = jnp.where(lane_head == col, 1.0, 0.0).astype(BF16)
    n_iota = lax.broadcasted_iota(jnp.int32, (nb, LANES), 0)
    sel_ref[...] = jnp.zeros_like(sel_ref)
    for b in range(bsz):
        p_hi, p_mid, p_lo = _split3(km_ref[b] * q_ref[b:b + 1, :])
        g = _dot(p_hi, ind) + _dot(p_mid, ind) + _dot(p_lo, ind)
        for r in range(MOBA_TOPK):
            m = jnp.max(g, axis=0, keepdims=True)
            idx = jnp.min(jnp.where(g == m, n_iota, nb), axis=0, keepdims=True)
            sel_ref[b, r:r + 1, :] = idx
            g = jnp.where(n_iota == idx, -jnp.inf, g)


def _sample_gate(q, kmean):
    bsz = q.shape[0]
    sel = pl.pallas_call(
        _sample_gate_kernel,
        out_shape=jax.ShapeDtypeStruct((bsz, 8, LANES), jnp.int32),
    )(q, kmean)
    return sel[:, :MOBA_TOPK, :N_HEADS]


def _sample_attn_kernel(sel_ref, pt_ref, q_ref, kn_ref, vn_ref, ck_ref, cv_ref, o_ref,
                        kbuf, vbuf, sem, *, n_pages, pages_per_block):
    b = pl.program_id(0)
    page_rows = ck_ref.shape[2]

    def copies(h, r, p):
        blk = sel_ref[(b * MOBA_TOPK + r) * N_HEADS + h]
        page = pt_ref[b * n_pages + blk * pages_per_block + p]
        row0 = (r * pages_per_block + p) * page_rows
        dst = pl.ds(row0, page_rows)
        return (pltpu.make_async_copy(ck_ref.at[0, page, :, h, :], kbuf.at[h, dst, :], sem.at[0]),
                pltpu.make_async_copy(cv_ref.at[0, page, :, h, :], vbuf.at[h, dst, :], sem.at[1]))

    triples = [(h, r, p) for h in range(N_HEADS) for r in range(MOBA_TOPK)
               for p in range(pages_per_block)]
    for hrp in triples:
        ck, cv = copies(*hrp)
        ck.start()
        cv.start()
    for hrp in triples:
        ck, cv = copies(*hrp)
        ck.wait()
        cv.wait()

    scale = HEAD_DIM ** -0.5
    q = q_ref[0] * scale
    kn = kn_ref[0]
    vn = vn_ref[0]
    s_self = jnp.sum(q * kn, axis=1, keepdims=True)
    for h in range(N_HEADS):
        qh = jnp.broadcast_to(q[h:h + 1], (8, HEAD_DIM)).astype(BF16)
        s = _dot_nt(qh, kbuf[h].astype(BF16))
        ss = s_self[h:h + 1]
        m = jnp.maximum(jnp.max(s, axis=1, keepdims=True), ss)
        p = jnp.exp(s - m)
        p_self = jnp.exp(ss - m)
        denom = jnp.sum(p, axis=1, keepdims=True) + p_self
        o = _dot(p.astype(BF16), vbuf[h].astype(BF16)) + p_self * vn[h:h + 1]
        o_ref[0, h:h + 1, :] = (o / denom)[0:1]


def _sample_attention(sel, page_table, q, k_new, v_new, cache_k, cache_v):
    bsz, n_pages = page_table.shape
    page_rows = cache_k.shape[2]
    ppb = MOBA_BLOCK // page_rows
    n_keys = MOBA_TOPK * MOBA_BLOCK
    hrow = pl.BlockSpec((1, N_HEADS, HEAD_DIM), lambda b, s, pt: (b, 0, 0))
    return pl.pallas_call(
        functools.partial(_sample_attn_kernel, n_pages=n_pages, pages_per_block=ppb),
        grid_spec=pltpu.PrefetchScalarGridSpec(
            num_scalar_prefetch=2, grid=(bsz,),
            in_specs=[hrow, hrow, hrow,
                      pl.BlockSpec(memory_space=pl.ANY), pl.BlockSpec(memory_space=pl.ANY)],
            out_specs=hrow,
            scratch_shapes=[pltpu.VMEM((N_HEADS, n_keys, HEAD_DIM), F32),
                            pltpu.VMEM((N_HEADS, n_keys, HEAD_DIM), F32),
                            pltpu.SemaphoreType.DMA((2,))]),
        out_shape=jax.ShapeDtypeStruct((bsz, N_HEADS, HEAD_DIM), F32),
        compiler_params=_cparams("arbitrary"),
    )(sel.reshape(-1), page_table.reshape(-1),
      q.reshape(bsz, N_HEADS, HEAD_DIM), k_new.reshape(bsz, N_HEADS, HEAD_DIM),
      v_new.reshape(bsz, N_HEADS, HEAD_DIM), cache_k, cache_v)


def _sample_hgrn_kernel(s0_ref, q_ref, kk_ref, i_ref, sg_ref, g_ref, s_ref, o_ref):
    eye = (lax.broadcasted_iota(jnp.int32, (HEAD_DIM, HEAD_DIM), 0)
           == lax.broadcasted_iota(jnp.int32, (HEAD_DIM, HEAD_DIM), 1))

    def column(row):
        return jnp.sum(jnp.where(eye, jnp.broadcast_to(row, (HEAD_DIM, HEAD_DIM)), 0.0),
                       axis=1, keepdims=True)

    for h in range(N_HEADS):
        kk = kk_ref[0, h:h + 1, :]
        s_new = column(1.0 - kk) * s0_ref[0, h] + column(kk) * i_ref[0, h:h + 1, :]
        s_ref[0, h] = s_new
        o = jnp.sum(column(q_ref[0, h:h + 1, :]) * s_new, axis=0, keepdims=True)
        o_ref[0, h:h + 1, :] = _rms(o, g_ref[...]) * sg_ref[0, h:h + 1, :]


def _sample_hgrn(state, qr, kk, ir, sg, g):
    bsz = state.shape[0]
    hrow = pl.BlockSpec((1, N_HEADS, HEAD_DIM), lambda b: (b, 0, 0))
    sblk = pl.BlockSpec((1, N_HEADS, HEAD_DIM, HEAD_DIM), lambda b: (b, 0, 0, 0))
    r3 = lambda a: a.reshape(bsz, N_HEADS, HEAD_DIM)
    return pl.pallas_call(
        _sample_hgrn_kernel,
        grid=(bsz,),
        in_specs=[sblk, hrow, hrow, hrow, hrow, pl.BlockSpec((1, HEAD_DIM), lambda b: (0, 0))],
        out_specs=[sblk, hrow],
        out_shape=[jax.ShapeDtypeStruct(state.shape, F32),
                   jax.ShapeDtypeStruct((bsz, N_HEADS, HEAD_DIM), F32)],
        compiler_params=_cparams("parallel"),
    )(state, r3(qr), r3(kk), r3(ir), r3(sg), g)


def kernel(x_prompt, x_sample, cache_k, cache_v, state_hgrn, page_table, norm1_g, w_in, lb_logits,
           hgrn_norm_g, w_out, norm2_g, w_up, w_down, final_norm_g):
    depth = w_in.shape[0]
    assert depth == 1, "single-layer trunk"
    bp, t, _ = x_prompt.shape
    bs, ts, _ = x_sample.shape
    n_pool, page_rows = cache_k.shape[1], cache_k.shape[2]
    n_pages = page_table.shape[1]
    past = n_pages * page_rows
    assert ts == 1 and t % 512 == 0
    assert past % MOBA_BLOCK == 0 and past // MOBA_BLOCK >= MOBA_TOPK and MOBA_BLOCK % page_rows == 0

    lb = jnp.cumsum(jax.nn.softmax(lb_logits.astype(F32), axis=0), axis=0)[0].reshape(1, D_HEADS)
    g1 = norm1_g[0].reshape(1, D_MODEL)
    n2 = norm2_g[0].reshape(1, D_MODEL)
    gf = final_norm_g.reshape(1, D_MODEL)
    hg = hgrn_norm_g[0].reshape(1, HEAD_DIM)
    w_in_bf = w_in[0].astype(BF16)
    w_out_bf = w_out[0].astype(BF16)
    w_up_bf = w_up[0].astype(BF16)
    w_down_bf = w_down[0].astype(BF16)

    tm = 512
    xp = x_prompt.reshape(bp * t, D_MODEL)
    cs_p = _rope_tables(jnp.arange(t, dtype=jnp.int32))
    (q_p, krow_p, vrow_p, kb_p, vt_p, kmean_p, qr_p, lf_p, kk_p, ir_p, sg_p) = _inproj(
        xp, g1, w_in_bf, cs_p, lb, tm=tm, n_pos_tiles=t // tm, prompt=True, batch=bp)
    kmean_p = kmean_p.reshape(bp, t // MOBA_BLOCK, D_HEADS)
    qt_p, bias_p = _gate(q_p, kmean_p, batch=bp, tq=512)
    oatt_p = _attention(qt_p, bias_p, kb_p, vt_p, batch=bp)
    orec_p, st_p = _hgrn(qr_p, lf_p, kk_p, ir_p, sg_p, jnp.tile(hg, (1, REC_GROUP // HEAD_DIM)), batch=bp)
    y_p = _ffn(xp, oatt_p, orec_p, w_out_bf, n2, w_up_bf, w_down_bf, gf, tm=512, tf=1024)

    xs = x_sample.reshape(bs, D_MODEL)
    cs_s = _rope_tables(jnp.full((bs,), past, jnp.int32))
    (q_s, krow_s, vrow_s, qr_s, lf_s, kk_s, ir_s, sg_s) = _inproj(
        xs, g1, w_in_bf, cs_s, lb, tm=bs, n_pos_tiles=1, prompt=False)
    kmean_s = _page_means(cache_k, page_table).reshape(bs, past // MOBA_BLOCK, D_HEADS)
    sel_s = _sample_gate(q_s, kmean_s)
    oatt_s = _sample_attention(sel_s, page_table, q_s, krow_s, vrow_s, cache_k, cache_v)
    st_s, orec_s = _sample_hgrn(state_hgrn[0], qr_s, kk_s, ir_s, sg_s, hg)
    y_s = _ffn(xs, oatt_s.reshape(bs, D_HEADS).astype(BF16), orec_s.reshape(bs, D_HEADS).astype(BF16),
               w_out_bf, n2, w_up_bf, w_down_bf, gf, tm=bs, tf=1024)

    heads = (N_HEADS, HEAD_DIM)
    return (y_p.reshape(bp, t, D_MODEL),
            y_s.reshape(bs, 1, D_MODEL),
            krow_p.reshape(1, bp, t, *heads),
            vrow_p.reshape(1, bp, t, *heads),
            _unpack_state(st_p)[None],
            krow_s.reshape(1, bs, 1, *heads),
            vrow_s.reshape(1, bs, 1, *heads),
            st_s[None])
```
